```python
import jax, jax.numpy as jnp
from jax import lax
import numpy as np

D_MODEL = 1024
BATCH = 4
SEQ = 8192
DEPTH = 2

N_MIXERS = 2
EXPAND = 2
D_INNER = EXPAND * D_MODEL
K_SHORT = 3
K_CONF = 31
N_META = 16
N_A = (DEPTH + 1) // 2
N_B = DEPTH // 2
RMS_EPS = 1e-6
LN_EPS = 1e-5

kernel_name = "hybrid_shortconv_conformer_trunk"


def rmsnorm(x, g):
    xf = x.astype(jnp.float32)
    y = xf * lax.rsqrt(jnp.mean(xf * xf, axis=-1, keepdims=True) + RMS_EPS)
    return (y * g.astype(jnp.float32)).astype(x.dtype)


def layernorm(x, g, b):
    xf = x.astype(jnp.float32)
    mu = jnp.mean(xf, axis=-1, keepdims=True)
    var = jnp.mean(jnp.square(xf - mu), axis=-1, keepdims=True)
    y = (xf - mu) * lax.rsqrt(var + LN_EPS)
    return (y * g.astype(jnp.float32) + b.astype(jnp.float32)).astype(x.dtype)


def causal_dwconv(u, w, b):
    K, C = w.shape
    out = lax.conv_general_dilated(
        u, w[:, None, :].astype(u.dtype),
        window_strides=(1,), padding=[(K - 1, 0)],
        dimension_numbers=("NWC", "WIO", "NWC"),
        feature_group_count=C)
    return out + b.astype(u.dtype)


def mixer_short_conv(h, w_in, conv_w, conv_b, w_out):
    proj = jnp.einsum("btd,de->bte", h, w_in)
    bg, cg, xv, z = jnp.split(proj, 4, axis=-1)
    v = bg * causal_dwconv(cg * xv, conv_w, conv_b)
    return jnp.einsum("bte,ed->btd", jax.nn.silu(z) * v, w_out)


def mixer_conformer_conv(h, w_in, conv_w, conv_b, ln_g, ln_b, w_out):
    proj = jnp.einsum("btd,de->bte", h, w_in)
    ua, ub, z = jnp.split(proj, 3, axis=-1)
    u = ua * jax.nn.sigmoid(ub)
    v = jax.nn.silu(layernorm(causal_dwconv(u, conv_w, conv_b), ln_g, ln_b))
    return jnp.einsum("bte,ed->btd", jax.nn.silu(z) * v, w_out)


def setup_inputs(seed: int = 0) -> dict:
    key = jax.random.key(seed)
    ks = jax.random.split(key, 16)
    f32 = jnp.float32
    n = lambda k, s, sc: jax.random.normal(k, s, f32) * sc
    return {
        "x": n(ks[0], (BATCH, SEQ, D_MODEL), 1.0),
        "meta": n(ks[1], (N_META, D_MODEL), 1.0),
        "norm_g": 1.0 + n(ks[2], (DEPTH, D_MODEL), 0.05),
        "a_w_in": n(ks[3], (N_A, D_MODEL, 4 * D_INNER), D_MODEL ** -0.5),
        "a_conv_w": n(ks[4], (N_A, K_SHORT, D_INNER), K_SHORT ** -0.5),
        "a_conv_b": n(ks[5], (N_A, D_INNER), 0.02),
        "a_w_out": n(ks[6], (N_A, D_INNER, D_MODEL), D_INNER ** -0.5),
        "b_w_in": n(ks[7], (N_B, D_MODEL, 3 * D_INNER), D_MODEL ** -0.5),
        "b_conv_w": n(ks[8], (N_B, K_CONF, D_INNER), K_CONF ** -0.5),
        "b_conv_b": n(ks[9], (N_B, D_INNER), 0.02),
        "b_ln_g": 1.0 + n(ks[10], (N_B, D_INNER), 0.05),
        "b_ln_b": n(ks[11], (N_B, D_INNER), 0.02),
        "b_w_out": n(ks[12], (N_B, D_INNER, D_MODEL), D_INNER ** -0.5),
        "final_g": 1.0 + n(ks[13], (D_MODEL,), 0.05),
    }


def reference(x, meta, norm_g, a_w_in, a_conv_w, a_conv_b, a_w_out,
              b_w_in, b_conv_w, b_conv_b, b_ln_g, b_ln_b, b_w_out, final_g):
    bsz = x.shape[0]
    meta_b = jnp.broadcast_to(meta.astype(x.dtype)[None], (bsz, N_META, D_MODEL))
    h = jnp.concatenate([meta_b, x], axis=1)
    for i in range(DEPTH):
        hn = rmsnorm(h, norm_g[i])
        j = i // N_MIXERS
        if i % N_MIXERS == 0:
            y = mixer_short_conv(hn, a_w_in[j], a_conv_w[j], a_conv_b[j], a_w_out[j])
        else:
            y = mixer_conformer_conv(hn, b_w_in[j], b_conv_w[j], b_conv_b[j],
                                     b_ln_g[j], b_ln_b[j], b_w_out[j])
        h = h + y
    return rmsnorm(h, final_g)[:, N_META:]
```

```python
import functools

import jax
import jax.numpy as jnp
from jax import lax
from jax.experimental import pallas as pl
from jax.experimental.pallas import tpu as pltpu

RMS_EPS = 1e-6
LN_EPS = 1e-5

SUBLANES = 8
LANES = 128
HIST_A = 8
HIST_B = 32
CHUNK = 512
CHUNK_COLS = CHUNK // LANES
VMEM_LIMIT = 56 * 1024 * 1024


def _rmsnorm(x, g):
    return x * lax.rsqrt(jnp.mean(x * x, axis=-1, keepdims=True) + RMS_EPS) * g


def _silu(x):
    return x * jax.nn.sigmoid(x)


def _lane_col(v, l):
    return v[:, l * LANES:(l + 1) * LANES]


def _init_history(t, ubuf, state_in_ref, hist):
    @pl.when(t == 0)
    def _():
        ubuf[:, 0:hist, :] = state_in_ref[...]


def _roll_history(ubuf, state_out_ref, tm, hist):
    last = ubuf[:, tm:tm + hist, :]
    ubuf[:, 0:hist, :] = last
    state_out_ref[0] = last


def _layer_a_kernel(h_ref, state_in_ref, g_ref, w_in_ref, cw_ref, cb_ref, w_out_ref,
                    out_ref, state_out_ref, ubuf, *, tm, c_inner):
    t = pl.program_id(1)
    _init_history(t, ubuf, state_in_ref, HIST_A)
    k_taps = cw_ref.shape[1]
    h = h_ref[0]
    hn = _rmsnorm(h, g_ref[...]).astype(jnp.bfloat16)
    y = jnp.zeros_like(h)
    for j in range(c_inner // CHUNK):

        def proj(p):
            w = w_in_ref[:, p * c_inner + j * CHUNK:p * c_inner + (j + 1) * CHUNK]
            return jnp.dot(hn, w, preferred_element_type=jnp.float32)

        bg, cg, xv, z = proj(0), proj(1), proj(2), proj(3)
        cx = cg * xv
        gated = []
        for l in range(CHUNK_COLS):
            col = j * CHUNK_COLS + l
            ubuf[col, HIST_A:HIST_A + tm, :] = _lane_col(cx, l)
            conv = jnp.broadcast_to(cb_ref[col], (tm, LANES))
            for k in range(k_taps):
                off = HIST_A - (k_taps - 1) + k
                conv = conv + ubuf[col, off:off + tm, :] * cw_ref[col, k:k + 1, :]
            gated.append(_silu(_lane_col(z, l)) * (_lane_col(bg, l) * conv))
        gated = jnp.concatenate(gated, axis=-1).astype(jnp.bfloat16)
        y = y + jnp.dot(gated, w_out_ref[j * CHUNK:(j + 1) * CHUNK, :],
                        preferred_element_type=jnp.float32)
    out_ref[0] = h + y
    _roll_history(ubuf, state_out_ref, tm, HIST_A)


def _conv_cols(ubuf, cw_ref, cb_ref, cbuf, *, tm, row_block):
    n_cols, k_taps, _ = cw_ref.shape
    n_sub = row_block // SUBLANES

    def body(col, carry):
        bias = jnp.broadcast_to(cb_ref[col], (SUBLANES, LANES))
        for rb in range(tm // row_block):
            acc = [bias for _ in range(n_sub)]
            for k in range(k_taps):
                w = jnp.broadcast_to(cw_ref[col, k:k + 1, :], (SUBLANES, LANES))
                for i in range(n_sub):
                    off = rb * row_block + i * SUBLANES + HIST_B - (k_taps - 1) + k
                    acc[i] = acc[i] + ubuf[col, off:off + SUBLANES, :] * w
            for i in range(n_sub):
                row = rb * row_block + i * SUBLANES
                cbuf[col, row:row + SUBLANES, :] = acc[i]
        return carry

    lax.fori_loop(0, n_cols, body, 0)


def _layer_b_kernel(h_ref, state_in_ref, g_ref, w_in_ref, cw_ref, cb_ref, ln_g_ref, ln_b_ref,
                    w_out_ref, fg_ref, out_ref, state_out_ref, ubuf, cbuf,
                    *, tm, c_inner, row_block, final_norm):
    t = pl.program_id(1)
    _init_history(t, ubuf, state_in_ref, HIST_B)
    h = h_ref[0]
    hn = _rmsnorm(h, g_ref[...]).astype(jnp.bfloat16)
    n_chunks = c_inner // CHUNK

    def proj(p, j):
        w = w_in_ref[:, p * c_inner + j * CHUNK:p * c_inner + (j + 1) * CHUNK]
        return jnp.dot(hn, w, preferred_element_type=jnp.float32)

    for j in range(n_chunks):
        u = proj(0, j) * jax.nn.sigmoid(proj(1, j))
        for l in range(CHUNK_COLS):
            ubuf[j * CHUNK_COLS + l, HIST_B:HIST_B + tm, :] = _lane_col(u, l)

    _conv_cols(ubuf, cw_ref, cb_ref, cbuf, tm=tm, row_block=row_block)

    c = cbuf[...]
    mu = jnp.sum(jnp.sum(c, axis=0), axis=-1, keepdims=True) / c_inner
    var = jnp.sum(jnp.sum(jnp.square(c - mu), axis=0), axis=-1, keepdims=True) / c_inner
    rstd = lax.rsqrt(var + LN_EPS)

    y = jnp.zeros_like(h)
    for j in range(n_chunks):
        z = proj(2, j)
        gated = []
        for l in range(CHUNK_COLS):
            col = j * CHUNK_COLS + l
            normed = (cbuf[col] - mu) * rstd * ln_g_ref[col] + ln_b_ref[col]
            gated.append(_silu(_lane_col(z, l)) * _silu(normed))
        gated = jnp.concatenate(gated, axis=-1).astype(jnp.bfloat16)
        y = y + jnp.dot(gated, w_out_ref[j * CHUNK:(j + 1) * CHUNK, :],
                        preferred_element_type=jnp.float32)
    out = h + y
    if final_norm:
        out = _rmsnorm(out, fg_ref[...])
    out_ref[0] = out
    _roll_history(ubuf, state_out_ref, tm, HIST_B)


def _const_spec(shape):
    return pl.BlockSpec(shape, lambda b, t: (0,) * len(shape), pipeline_mode=pl.Buffered(1))


def _run_layer(body, h, state_in, consts, scratch, *, tm, name):
    bsz, seq, d = h.shape
    tok_spec = pl.BlockSpec((1, tm, d), lambda b, t: (b, t, 0))
    state_shape = (bsz,) + state_in.shape
    return pl.pallas_call(
        body,
        grid=(bsz, seq // tm),
        in_specs=[tok_spec, _const_spec(state_in.shape)] + [_const_spec(a.shape) for a in consts],
        out_specs=[tok_spec, pl.BlockSpec((1,) + state_in.shape, lambda b, t: (b, 0, 0, 0))],
        out_shape=[jax.ShapeDtypeStruct(h.shape, h.dtype),
                   jax.ShapeDtypeStruct(state_shape, jnp.float32)],
        scratch_shapes=scratch,
        compiler_params=pltpu.CompilerParams(
            dimension_semantics=("arbitrary", "arbitrary"), vmem_limit_bytes=VMEM_LIMIT),
        name=name,
    )(h, state_in, *consts)


def _layer_a(h, state_in, g, w_in, cw, cb, w_out, *, tm, name):
    n_cols = cw.shape[0]
    body = functools.partial(_layer_a_kernel, tm=tm, c_inner=n_cols * LANES)
    scratch = [pltpu.VMEM((n_cols, HIST_A + tm, LANES), jnp.float32)]
    return _run_layer(body, h, state_in, (g, w_in, cw, cb, w_out), scratch, tm=tm, name=name)


def _layer_b(h, state_in, g, w_in, cw, cb, ln_g, ln_b, w_out, fg, *, tm, row_block,
             final_norm, name):
    n_cols = cw.shape[0]
    body = functools.partial(_layer_b_kernel, tm=tm, c_inner=n_cols * LANES,
                             row_block=row_block, final_norm=final_norm)
    scratch = [pltpu.VMEM((n_cols, HIST_B + tm, LANES), jnp.float32),
               pltpu.VMEM((n_cols, tm, LANES), jnp.float32)]
    return _run_layer(body, h, state_in, (g, w_in, cw, cb, ln_g, ln_b, w_out, fg), scratch,
                      tm=tm, name=name)


def _by_lane_col(v):
    lead = v.shape[:-1]
    v = v.reshape(lead + (v.shape[-1] // LANES, LANES))
    return jnp.moveaxis(v, -2, 0)


def kernel(x, meta, norm_g, a_w_in, a_conv_w, a_conv_b, a_w_out, b_w_in, b_conv_w, b_conv_b,
           b_ln_g, b_ln_b, b_w_out, final_g):
    n_meta = meta.shape[0]
    c_inner = a_conv_w.shape[-1]
    n_cols = c_inner // LANES
    tm = 256
    bf16 = jnp.bfloat16
    row = lambda v: v.reshape(1, -1)
    col_row = lambda v: _by_lane_col(row(v))

    a_consts = (row(norm_g[0]), a_w_in[0].astype(bf16), _by_lane_col(a_conv_w[0]),
                col_row(a_conv_b[0]), a_w_out[0].astype(bf16))
    b_consts = (row(norm_g[1]), b_w_in[0].astype(bf16), _by_lane_col(b_conv_w[0]),
                col_row(b_conv_b[0]), col_row(b_ln_g[0]), col_row(b_ln_b[0]),
                b_w_out[0].astype(bf16), row(final_g))

    zeros_a = jnp.zeros((n_cols, HIST_A, LANES), jnp.float32)
    zeros_b = jnp.zeros((n_cols, HIST_B, LANES), jnp.float32)
    meta_h = meta.astype(x.dtype)[None]
    meta_h1, meta_state_a = _layer_a(meta_h, zeros_a, *a_consts, tm=n_meta, name="meta_layer_a")
    _, meta_state_b = _layer_b(meta_h1, zeros_b, *b_consts, tm=n_meta, row_block=n_meta,
                               final_norm=False, name="meta_layer_b")

    h1, _ = _layer_a(x, meta_state_a[0], *a_consts, tm=tm, name="layer_a")
    out, _ = _layer_b(h1, meta_state_b[0], *b_consts, tm=tm, row_block=32,
                      final_norm=True, name="layer_b")
    return out
```

```python
import functools

import jax
import jax.numpy as jnp
from jax import lax
from jax.experimental import pallas as pl
from jax.experimental.pallas import tpu as pltpu

RMS_EPS = 1e-6
LN_EPS = 1e-5

SUBLANES = 8
LANES = 128
HIST_A = 8
HIST_B = 32
CHUNK = 512
CHUNK_COLS = CHUNK // LANES
VMEM_LIMIT = 56 * 1024 * 1024


def _rmsnorm(x, g):
    return x * lax.rsqrt(jnp.mean(x * x, axis=-1, keepdims=True) + RMS_EPS) * g


def _silu(x):
    return x * jax.nn.sigmoid(x)


def _lane_col(v, l):
    return v[:, l * LANES:(l + 1) * LANES]


def _init_history(t, ubuf, state_in_ref, hist):
    @pl.when(t == 0)
    def _():
        ubuf[:, 0:hist, :] = state_in_ref[...]


def _roll_history(ubuf, state_out_ref, tm, hist):
    last = ubuf[:, tm:tm + hist, :]
    ubuf[:, 0:hist, :] = last
    state_out_ref[0] = last


def _layer_a_kernel(h_ref, state_in_ref, g_ref, w_in_ref, cw_ref, cb_ref, w_out_ref,
                    out_ref, state_out_ref, ubuf, *, tm, c_inner):
    t = pl.program_id(1)
    _init_history(t, ubuf, state_in_ref, HIST_A)
    k_taps = cw_ref.shape[1]
    h = h_ref[0]
    hn = _rmsnorm(h, g_ref[...]).astype(jnp.bfloat16)
    y = jnp.zeros_like(h)
    for j in range(c_inner // CHUNK):

        def proj(p):
            w = w_in_ref[:, p * c_inner + j * CHUNK:p * c_inner + (j + 1) * CHUNK]
            return jnp.dot(hn, w, preferred_element_type=jnp.float32)

        bg, cg, xv, z = proj(0), proj(1), proj(2), proj(3)
        cx = cg * xv
        gated = []
        for l in range(CHUNK_COLS):
            col = j * CHUNK_COLS + l
            ubuf[col, HIST_A:HIST_A + tm, :] = _lane_col(cx, l)
            conv = jnp.broadcast_to(cb_ref[col], (tm, LANES))
            for k in range(k_taps):
                off = HIST_A - (k_taps - 1) + k
                conv = conv + ubuf[col, off:off + tm, :] * cw_ref[col, k:k + 1, :]
            gated.append(_silu(_lane_col(z, l)) * (_lane_col(bg, l) * conv))
        gated = jnp.concatenate(gated, axis=-1).astype(jnp.bfloat16)
        y = y + jnp.dot(gated, w_out_ref[j * CHUNK:(j + 1) * CHUNK, :],
                        preferred_element_type=jnp.float32)
    out_ref[0] = h + y
    _roll_history(ubuf, state_out_ref, tm, HIST_A)


def _conv_col(ubuf, cw_ref, cb_ref, cbuf, stat, col, *, tm, row_block):
    k_taps = cw_ref.shape[1]
    n_sub = row_block // SUBLANES
    bias = jnp.broadcast_to(cb_ref[col], (SUBLANES, LANES))
    for rb in range(tm // row_block):
        acc = [bias for _ in range(n_sub)]
        for k in range(k_taps):
            w = jnp.broadcast_to(cw_ref[col, k:k + 1, :], (SUBLANES, LANES))
            for i in range(n_sub):
                off = rb * row_block + i * SUBLANES + HIST_B - (k_taps - 1) + k
                acc[i] = acc[i] + ubuf[col, off:off + SUBLANES, :] * w
        for i in range(n_sub):
            rows = slice(rb * row_block + i * SUBLANES, rb * row_block + (i + 1) * SUBLANES)
            cbuf[col, rows, :] = acc[i]
            if col == 0:
                stat[0, rows, :] = acc[i]
                stat[1, rows, :] = acc[i] * acc[i]
            else:
                stat[0, rows, :] += acc[i]
                stat[1, rows, :] += acc[i] * acc[i]


def _layer_b_kernel(h_ref, state_in_ref, g_ref, w_in_ref, cw_ref, cb_ref, ln_g_ref, ln_b_ref,
                    w_out_ref, fg_ref, out_ref, state_out_ref, ubuf, cbuf, stat, zbuf, gbuf,
                    *, tm, c_inner, row_block, final_norm):
    t = pl.program_id(1)
    _init_history(t, ubuf, state_in_ref, HIST_B)
    h = h_ref[0]
    hn = _rmsnorm(h, g_ref[...]).astype(jnp.bfloat16)
    n_chunks = c_inner // CHUNK

    def proj(p, j):
        w = w_in_ref[:, p * c_inner + j * CHUNK:p * c_inner + (j + 1) * CHUNK]
        return jnp.dot(hn, w, preferred_element_type=jnp.float32)

    for j in range(n_chunks):
        u = proj(0, j) * jax.nn.sigmoid(proj(1, j))
        for l in range(CHUNK_COLS):
            ubuf[j * CHUNK_COLS + l, HIST_B:HIST_B + tm, :] = _lane_col(u, l)
        zbuf[:, j * CHUNK:(j + 1) * CHUNK] = _silu(proj(2, j))
        for l in range(CHUNK_COLS):
            _conv_col(ubuf, cw_ref, cb_ref, cbuf, stat, j * CHUNK_COLS + l,
                      tm=tm, row_block=row_block)

    mu = jnp.sum(stat[0], axis=-1, keepdims=True) / c_inner
    var = jnp.maximum(jnp.sum(stat[1], axis=-1, keepdims=True) / c_inner - mu * mu, 0.0)
    rstd = lax.rsqrt(var + LN_EPS)

    for col in range(c_inner // LANES):
        lanes = slice(col * LANES, (col + 1) * LANES)
        normed = (cbuf[col] - mu) * rstd * ln_g_ref[col] + ln_b_ref[col]
        gbuf[:, lanes] = (zbuf[:, lanes] * _silu(normed)).astype(jnp.bfloat16)
    out = h + jnp.dot(gbuf[...], w_out_ref[...], preferred_element_type=jnp.float32)
    if final_norm:
        out = _rmsnorm(out, fg_ref[...])
    out_ref[0] = out
    _roll_history(ubuf, state_out_ref, tm, HIST_B)


def _const_spec(shape):
    return pl.BlockSpec(shape, lambda b, t: (0,) * len(shape), pipeline_mode=pl.Buffered(1))


def _run_layer(body, h, state_in, consts, scratch, *, tm, name):
    bsz, seq, d = h.shape
    tok_spec = pl.BlockSpec((1, tm, d), lambda b, t: (b, t, 0))
    state_shape = (bsz,) + state_in.shape
    return pl.pallas_call(
        body,
        grid=(bsz, seq // tm),
        in_specs=[tok_spec, _const_spec(state_in.shape)] + [_const_spec(a.shape) for a in consts],
        out_specs=[tok_spec, pl.BlockSpec((1,) + state_in.shape, lambda b, t: (b, 0, 0, 0))],
        out_shape=[jax.ShapeDtypeStruct(h.shape, h.dtype),
                   jax.ShapeDtypeStruct(state_shape, jnp.float32)],
        scratch_shapes=scratch,
        compiler_params=pltpu.CompilerParams(
            dimension_semantics=("arbitrary", "arbitrary"), vmem_limit_bytes=VMEM_LIMIT),
        name=name,
    )(h, state_in, *consts)


def _layer_a(h, state_in, g, w_in, cw, cb, w_out, *, tm, name):
    n_cols = cw.shape[0]
    body = functools.partial(_layer_a_kernel, tm=tm, c_inner=n_cols * LANES)
    scratch = [pltpu.VMEM((n_cols, HIST_A + tm, LANES), jnp.float32)]
    return _run_layer(body, h, state_in, (g, w_in, cw, cb, w_out), scratch, tm=tm, name=name)


def _layer_b(h, state_in, g, w_in, cw, cb, ln_g, ln_b, w_out, fg, *, tm, row_block,
             final_norm, name):
    n_cols = cw.shape[0]
    body = functools.partial(_layer_b_kernel, tm=tm, c_inner=n_cols * LANES,
                             row_block=row_block, final_norm=final_norm)
    scratch = [pltpu.VMEM((n_cols, HIST_B + tm, LANES), jnp.float32),
               pltpu.VMEM((n_cols, tm, LANES), jnp.float32),
               pltpu.VMEM((2, tm, LANES), jnp.float32),
               pltpu.VMEM((tm, n_cols * LANES), jnp.float32),
               pltpu.VMEM((tm, n_cols * LANES), jnp.bfloat16)]
    return _run_layer(body, h, state_in, (g, w_in, cw, cb, ln_g, ln_b, w_out, fg), scratch,
                      tm=tm, name=name)


def _by_lane_col(v):
    lead = v.shape[:-1]
    v = v.reshape(lead + (v.shape[-1] // LANES, LANES))
    return jnp.moveaxis(v, -2, 0)


def kernel(x, meta, norm_g, a_w_in, a_conv_w, a_conv_b, a_w_out, b_w_in, b_conv_w, b_conv_b,
           b_ln_g, b_ln_b, b_w_out, final_g):
    n_meta = meta.shape[0]
    c_inner = a_conv_w.shape[-1]
    n_cols = c_inner // LANES
    tm = 256
    bf16 = jnp.bfloat16
    row = lambda v: v.reshape(1, -1)
    col_row = lambda v: _by_lane_col(row(v))

    a_consts = (row(norm_g[0]), a_w_in[0].astype(bf16), _by_lane_col(a_conv_w[0]),
                col_row(a_conv_b[0]), a_w_out[0].astype(bf16))
    b_consts = (row(norm_g[1]), b_w_in[0].astype(bf16), _by_lane_col(b_conv_w[0]),
                col_row(b_conv_b[0]), col_row(b_ln_g[0]), col_row(b_ln_b[0]),
                b_w_out[0].astype(bf16), row(final_g))

    zeros_a = jnp.zeros((n_cols, HIST_A, LANES), jnp.float32)
    zeros_b = jnp.zeros((n_cols, HIST_B, LANES), jnp.float32)
    meta_h = meta.astype(x.dtype)[None]
    meta_h1, meta_state_a = _layer_a(meta_h, zeros_a, *a_consts, tm=n_meta, name="meta_layer_a")
    _, meta_state_b = _layer_b(meta_h1, zeros_b, *b_consts, tm=n_meta, row_block=n_meta,
                               final_norm=False, name="meta_layer_b")

    h1, _ = _layer_a(x, meta_state_a[0], *a_consts, tm=tm, name="layer_a")
    out, _ = _layer_b(h1, meta_state_b[0], *b_consts, tm=tm, row_block=32,
                      final_norm=True, name="layer_b")
    return out
```

```python
import functools

import jax
import jax.numpy as jnp
from jax import lax
from jax.experimental import pallas as pl
from jax.experimental.pallas import tpu as pltpu

RMS_EPS = 1e-6
LN_EPS = 1e-5

SUBLANES = 8
LANES = 128
HIST_A = 8
HIST_B = 32
CHUNK = 512
CHUNK_COLS = CHUNK // LANES
VMEM_LIMIT = 56 * 1024 * 1024


def _rmsnorm(x, g):
    return x * lax.rsqrt(jnp.mean(x * x, axis=-1, keepdims=True) + RMS_EPS) * g


def _silu(x):
    return x * jax.nn.sigmoid(x)


def _lane_col(v, l):
    return v[:, l * LANES:(l + 1) * LANES]


def _init_history(t, ubuf, state_in_ref, hist):
    @pl.when(t == 0)
    def _():
        ubuf[:, 0:hist, :] = state_in_ref[...]


def _roll_history(ubuf, state_out_ref, tm, hist):
    last = ubuf[:, tm:tm + hist, :]
    ubuf[:, 0:hist, :] = last
    state_out_ref[0] = last


def _layer_a_kernel(h_ref, state_in_ref, g_ref, w_in_ref, cw_ref, cb_ref, w_out_ref,
                    out_ref, state_out_ref, ubuf, *, tm, c_inner):
    t = pl.program_id(1)
    _init_history(t, ubuf, state_in_ref, HIST_A)
    k_taps = cw_ref.shape[1]
    h = h_ref[0]
    hn = _rmsnorm(h, g_ref[...]).astype(jnp.bfloat16)
    y = jnp.zeros_like(h)
    for j in range(c_inner // CHUNK):

        def proj(p):
            w = w_in_ref[:, p * c_inner + j * CHUNK:p * c_inner + (j + 1) * CHUNK]
            return jnp.dot(hn, w, preferred_element_type=jnp.float32)

        bg, cg, xv, z = proj(0), proj(1), proj(2), proj(3)
        cx = cg * xv
        gated = []
        for l in range(CHUNK_COLS):
            col = j * CHUNK_COLS + l
            ubuf[col, HIST_A:HIST_A + tm, :] = _lane_col(cx, l)
            conv = jnp.broadcast_to(cb_ref[col], (tm, LANES))
            for k in range(k_taps):
                off = HIST_A - (k_taps - 1) + k
                conv = conv + ubuf[col, off:off + tm, :] * cw_ref[col, k:k + 1, :]
            gated.append(_silu(_lane_col(z, l)) * (_lane_col(bg, l) * conv))
        gated = jnp.concatenate(gated, axis=-1).astype(jnp.bfloat16)
        y = y + jnp.dot(gated, w_out_ref[j * CHUNK:(j + 1) * CHUNK, :],
                        preferred_element_type=jnp.float32)
    out_ref[0] = h + y
    _roll_history(ubuf, state_out_ref, tm, HIST_A)


def _bf16_round_bits(bits):
    return bits + jnp.uint32(0x7FFF) + ((bits >> 16) & jnp.uint32(1))


def _pack_bits(lo_bits, hi_bits):
    return (_bf16_round_bits(lo_bits) >> 16) | (_bf16_round_bits(hi_bits) & jnp.uint32(0xFFFF0000))


def _pack_pair(lo, hi):
    return _pack_bits(pltpu.bitcast(lo, jnp.uint32), pltpu.bitcast(hi, jnp.uint32))


def _unpack_pair(word):
    lo = pltpu.bitcast(word << 16, jnp.float32)
    hi = pltpu.bitcast(word & jnp.uint32(0xFFFF0000), jnp.float32)
    return lo, hi


def _conv_pair(ubuf, cw_ref, cb_ref, cbuf, stat, pair, *, tm, row_block):
    k_taps = cw_ref.shape[1]
    n_sub = row_block // SUBLANES
    for rb in range(tm // row_block):
        acc = [None] * n_sub
        for k in range(k_taps):
            w = pltpu.bitcast(jnp.broadcast_to(cw_ref[pair, k:k + 1, :], (SUBLANES, LANES)),
                              jnp.bfloat16)
            for i in range(n_sub):
                off = rb * row_block + i * SUBLANES + HIST_B - (k_taps - 1) + k
                prod = pltpu.bitcast(ubuf[pair, off:off + SUBLANES, :], jnp.bfloat16) * w
                acc[i] = prod if acc[i] is None else acc[i] + prod
        for i in range(n_sub):
            rows = slice(rb * row_block + i * SUBLANES, rb * row_block + (i + 1) * SUBLANES)
            for half, c in enumerate(_unpack_pair(pltpu.bitcast(acc[i], jnp.uint32))):
                col = 2 * pair + half
                c = c + cb_ref[col]
                cbuf[col, rows, :] = c
                if col == 0:
                    stat[0, rows, :] = c
                    stat[1, rows, :] = c * c
                else:
                    stat[0, rows, :] += c
                    stat[1, rows, :] += c * c


def _layer_b_kernel(h_ref, state_in_ref, g_ref, w_in_ref, cw_ref, cb_ref, ln_g_ref, ln_b_ref,
                    w_out_ref, fg_ref, out_ref, state_out_ref, ubuf, cbuf, stat, zbuf, gbuf,
                    *, tm, c_inner, row_block, final_norm):
    t = pl.program_id(1)
    _init_history(t, ubuf, state_in_ref, HIST_B)
    h = h_ref[0]
    hn = _rmsnorm(h, g_ref[...]).astype(jnp.bfloat16)
    n_chunks = c_inner // CHUNK
    chunk_pairs = CHUNK_COLS // 2

    def proj(p, j):
        w = w_in_ref[:, p * c_inner + j * CHUNK:p * c_inner + (j + 1) * CHUNK]
        return jnp.dot(hn, w, preferred_element_type=jnp.float32)

    for j in range(n_chunks):
        u = proj(0, j) * jax.nn.sigmoid(proj(1, j))
        for p in range(chunk_pairs):
            ubuf[j * chunk_pairs + p, HIST_B:HIST_B + tm, :] = _pack_pair(
                _lane_col(u, 2 * p), _lane_col(u, 2 * p + 1))
        zbuf[:, j * CHUNK:(j + 1) * CHUNK] = _silu(proj(2, j))
        for p in range(chunk_pairs):
            _conv_pair(ubuf, cw_ref, cb_ref, cbuf, stat, j * chunk_pairs + p,
                       tm=tm, row_block=row_block)

    mu = jnp.sum(stat[0], axis=-1, keepdims=True) / c_inner
    var = jnp.maximum(jnp.sum(stat[1], axis=-1, keepdims=True) / c_inner - mu * mu, 0.0)
    rstd = lax.rsqrt(var + LN_EPS)

    for col in range(c_inner // LANES):
        lanes = slice(col * LANES, (col + 1) * LANES)
        normed = (cbuf[col] - mu) * rstd * ln_g_ref[col] + ln_b_ref[col]
        gbuf[:, lanes] = (zbuf[:, lanes] * _silu(normed)).astype(jnp.bfloat16)
    out = h + jnp.dot(gbuf[...], w_out_ref[...], preferred_element_type=jnp.float32)
    if final_norm:
        out = _rmsnorm(out, fg_ref[...])
    out_ref[0] = out
    _roll_history(ubuf, state_out_ref, tm, HIST_B)


def _const_spec(shape):
    return pl.BlockSpec(shape, lambda b, t: (0,) * len(shape), pipeline_mode=pl.Buffered(1))


def _run_layer(body, h, state_in, consts, scratch, *, tm, name):
    bsz, seq, d = h.shape
    tok_spec = pl.BlockSpec((1, tm, d), lambda b, t: (b, t, 0))
    state_shape = (bsz,) + state_in.shape
    return pl.pallas_call(
        body,
        grid=(bsz, seq // tm),
        in_specs=[tok_spec, _const_spec(state_in.shape)] + [_const_spec(a.shape) for a in consts],
        out_specs=[tok_spec, pl.BlockSpec((1,) + state_in.shape, lambda b, t: (b, 0, 0, 0))],
        out_shape=[jax.ShapeDtypeStruct(h.shape, h.dtype),
                   jax.ShapeDtypeStruct(state_shape, state_in.dtype)],
        scratch_shapes=scratch,
        compiler_params=pltpu.CompilerParams(
            dimension_semantics=("arbitrary", "arbitrary"), vmem_limit_bytes=VMEM_LIMIT),
        name=name,
    )(h, state_in, *consts)


def _layer_a(h, state_in, g, w_in, cw, cb, w_out, *, tm, name):
    n_cols = cw.shape[0]
    body = functools.partial(_layer_a_kernel, tm=tm, c_inner=n_cols * LANES)
    scratch = [pltpu.VMEM((n_cols, HIST_A + tm, LANES), jnp.float32)]
    return _run_layer(body, h, state_in, (g, w_in, cw, cb, w_out), scratch, tm=tm, name=name)


def _layer_b(h, state_in, g, w_in, cw, cb, ln_g, ln_b, w_out, fg, *, tm, row_block,
             final_norm, name):
    n_cols = cb.shape[0]
    body = functools.partial(_layer_b_kernel, tm=tm, c_inner=n_cols * LANES,
                             row_block=row_block, final_norm=final_norm)
    scratch = [pltpu.VMEM((n_cols // 2, HIST_B + tm, LANES), jnp.uint32),
               pltpu.VMEM((n_cols, tm, LANES), jnp.float32),
               pltpu.VMEM((2, tm, LANES), jnp.float32),
               pltpu.VMEM((tm, n_cols * LANES), jnp.float32),
               pltpu.VMEM((tm, n_cols * LANES), jnp.bfloat16)]
    return _run_layer(body, h, state_in, (g, w_in, cw, cb, ln_g, ln_b, w_out, fg), scratch,
                      tm=tm, name=name)


def _by_lane_col(v):
    lead = v.shape[:-1]
    v = v.reshape(lead + (v.shape[-1] // LANES, LANES))
    return jnp.moveaxis(v, -2, 0)


def _pack_conv_taps(cw):
    bits = lax.bitcast_convert_type(_by_lane_col(cw), jnp.uint32)
    return _pack_bits(bits[0::2], bits[1::2])


def kernel(x, meta, norm_g, a_w_in, a_conv_w, a_conv_b, a_w_out, b_w_in, b_conv_w, b_conv_b,
           b_ln_g, b_ln_b, b_w_out, final_g):
    n_meta = meta.shape[0]
    c_inner = a_conv_w.shape[-1]
    n_cols = c_inner // LANES
    tm = 256
    bf16 = jnp.bfloat16
    row = lambda v: v.reshape(1, -1)
    col_row = lambda v: _by_lane_col(row(v))

    a_consts = (row(norm_g[0]), a_w_in[0].astype(bf16), _by_lane_col(a_conv_w[0]),
                col_row(a_conv_b[0]), a_w_out[0].astype(bf16))
    b_consts = (row(norm_g[1]), b_w_in[0].astype(bf16), _pack_conv_taps(b_conv_w[0]),
                col_row(b_conv_b[0]), col_row(b_ln_g[0]), col_row(b_ln_b[0]),
                b_w_out[0].astype(bf16), row(final_g))

    zeros_a = jnp.zeros((n_cols, HIST_A, LANES), jnp.float32)
    zeros_b = jnp.zeros((n_cols // 2, HIST_B, LANES), jnp.uint32)
    meta_h = meta.astype(x.dtype)[None]
    meta_h1, meta_state_a = _layer_a(meta_h, zeros_a, *a_consts, tm=n_meta, name="meta_layer_a")
    _, meta_state_b = _layer_b(meta_h1, zeros_b, *b_consts, tm=n_meta, row_block=n_meta,
                               final_norm=False, name="meta_layer_b")

    h1, _ = _layer_a(x, meta_state_a[0], *a_consts, tm=tm, name="layer_a")
    out, _ = _layer_b(h1, meta_state_b[0], *b_consts, tm=tm, row_block=32,
                      final_norm=True, name="layer_b")
    return out
```

```python
import functools

import jax
import jax.numpy as jnp
from jax import lax
from jax.experimental import pallas as pl
from jax.experimental.pallas import tpu as pltpu

RMS_EPS = 1e-6
LN_EPS = 1e-5

SUBLANES = 8
LANES = 128
HIST_A = 8
HIST_B = 32
CHUNK = 512
CHUNK_COLS = CHUNK // LANES
TILE_A = 512
TILE_B = 256
VMEM_LIMIT = 56 * 1024 * 1024


def _rmsnorm(x, g):
    return x * lax.rsqrt(jnp.mean(x * x, axis=-1, keepdims=True) + RMS_EPS) * g


def _silu(x):
    return x * jax.nn.sigmoid(x)


def _lane_col(v, l):
    return v[:, l * LANES:(l + 1) * LANES]


def _init_history(t, ubuf, state_in_ref, hist):
    @pl.when(t == 0)
    def _():
        ubuf[:, 0:hist, :] = state_in_ref[...]


def _roll_history(ubuf, state_out_ref, tm, hist):
    last = ubuf[:, tm:tm + hist, :]
    ubuf[:, 0:hist, :] = last
    state_out_ref[0] = last


def _layer_a_kernel(h_ref, state_in_ref, g_ref, w_in_ref, cw_ref, cb_ref, w_out_ref,
                    out_ref, state_out_ref, ubuf, *, tm, c_inner):
    t = pl.program_id(1)
    _init_history(t, ubuf, state_in_ref, HIST_A)
    k_taps = cw_ref.shape[1]
    h = h_ref[0]
    hn = _rmsnorm(h, g_ref[...]).astype(jnp.bfloat16)
    y = jnp.zeros_like(h)
    for j in range(c_inner // CHUNK):

        def proj(p):
            w = w_in_ref[:, p * c_inner + j * CHUNK:p * c_inner + (j + 1) * CHUNK]
            return jnp.dot(hn, w, preferred_element_type=jnp.float32)

        bg, cg, xv, z = proj(0), proj(1), proj(2), proj(3)
        cx = cg * xv
        gated = []
        for l in range(CHUNK_COLS):
            col = j * CHUNK_COLS + l
            ubuf[col, HIST_A:HIST_A + tm, :] = _lane_col(cx, l)
            conv = jnp.broadcast_to(cb_ref[col], (tm, LANES))
            for k in range(k_taps):
                off = HIST_A - (k_taps - 1) + k
                conv = conv + ubuf[col, off:off + tm, :] * cw_ref[col, k:k + 1, :]
            gated.append(_silu(_lane_col(z, l)) * (_lane_col(bg, l) * conv))
        gated = jnp.concatenate(gated, axis=-1).astype(jnp.bfloat16)
        y = y + jnp.dot(gated, w_out_ref[j * CHUNK:(j + 1) * CHUNK, :],
                        preferred_element_type=jnp.float32)
    out_ref[0] = h + y
    _roll_history(ubuf, state_out_ref, tm, HIST_A)


def _bf16_round_bits(bits):
    return bits + jnp.uint32(0x7FFF) + ((bits >> 16) & jnp.uint32(1))


def _pack_bits(lo_bits, hi_bits):
    return (_bf16_round_bits(lo_bits) >> 16) | (_bf16_round_bits(hi_bits) & jnp.uint32(0xFFFF0000))


def _pack_pair(lo, hi):
    return _pack_bits(pltpu.bitcast(lo, jnp.uint32), pltpu.bitcast(hi, jnp.uint32))


def _unpack_pair(word):
    lo = pltpu.bitcast(word << 16, jnp.float32)
    hi = pltpu.bitcast(word & jnp.uint32(0xFFFF0000), jnp.float32)
    return lo, hi


def _conv_pair(ubuf, cw_ref, cb_ref, cbuf, stat, pair, *, tm, row_block):
    k_taps = cw_ref.shape[1]
    n_sub = row_block // SUBLANES
    for rb in range(tm // row_block):
        acc = [None] * n_sub
        for k in range(k_taps):
            w = pltpu.bitcast(jnp.broadcast_to(cw_ref[pair, k:k + 1, :], (SUBLANES, LANES)),
                              jnp.bfloat16)
            for i in range(n_sub):
                off = rb * row_block + i * SUBLANES + HIST_B - (k_taps - 1) + k
                prod = pltpu.bitcast(ubuf[pair, off:off + SUBLANES, :], jnp.bfloat16) * w
                acc[i] = prod if acc[i] is None else acc[i] + prod
        for i in range(n_sub):
            rows = slice(rb * row_block + i * SUBLANES, rb * row_block + (i + 1) * SUBLANES)
            for half, c in enumerate(_unpack_pair(pltpu.bitcast(acc[i], jnp.uint32))):
                col = 2 * pair + half
                c = c + cb_ref[col]
                cbuf[col, rows, :] = c
                if col == 0:
                    stat[0, rows, :] = c
                    stat[1, rows, :] = c * c
                else:
                    stat[0, rows, :] += c
                    stat[1, rows, :] += c * c


def _layer_b_kernel(h_ref, state_in_ref, g_ref, w_in_ref, cw_ref, cb_ref, ln_g_ref, ln_b_ref,
                    w_out_ref, fg_ref, out_ref, state_out_ref, ubuf, cbuf, stat, zbuf, gbuf,
                    *, tm, c_inner, row_block, final_norm):
    t = pl.program_id(1)
    _init_history(t, ubuf, state_in_ref, HIST_B)
    h = h_ref[0]
    hn = _rmsnorm(h, g_ref[...]).astype(jnp.bfloat16)
    n_chunks = c_inner // CHUNK
    chunk_pairs = CHUNK_COLS // 2

    def proj(p, j):
        w = w_in_ref[:, p * c_inner + j * CHUNK:p * c_inner + (j + 1) * CHUNK]
        return jnp.dot(hn, w, preferred_element_type=jnp.float32)

    for j in range(n_chunks):
        u = proj(0, j) * jax.nn.sigmoid(proj(1, j))
        for p in range(chunk_pairs):
            ubuf[j * chunk_pairs + p, HIST_B:HIST_B + tm, :] = _pack_pair(
                _lane_col(u, 2 * p), _lane_col(u, 2 * p + 1))
        zbuf[:, j * CHUNK:(j + 1) * CHUNK] = _silu(proj(2, j))
        for p in range(chunk_pairs):
            _conv_pair(ubuf, cw_ref, cb_ref, cbuf, stat, j * chunk_pairs + p,
                       tm=tm, row_block=row_block)

    mu = jnp.sum(stat[0], axis=-1, keepdims=True) / c_inner
    var = jnp.maximum(jnp.sum(stat[1], axis=-1, keepdims=True) / c_inner - mu * mu, 0.0)
    rstd = lax.rsqrt(var + LN_EPS)

    for col in range(c_inner // LANES):
        lanes = slice(col * LANES, (col + 1) * LANES)
        normed = (cbuf[col] - mu) * rstd * ln_g_ref[col] + ln_b_ref[col]
        gbuf[:, lanes] = (zbuf[:, lanes] * _silu(normed)).astype(jnp.bfloat16)
    out = h + jnp.dot(gbuf[...], w_out_ref[...], preferred_element_type=jnp.float32)
    if final_norm:
        out = _rmsnorm(out, fg_ref[...])
    out_ref[0] = out
    _roll_history(ubuf, state_out_ref, tm, HIST_B)


def _const_spec(shape):
    return pl.BlockSpec(shape, lambda b, t: (0,) * len(shape), pipeline_mode=pl.Buffered(1))


def _run_layer(body, h, state_in, consts, scratch, *, tm, name):
    bsz, seq, d = h.shape
    tok_spec = pl.BlockSpec((1, tm, d), lambda b, t: (b, t, 0))
    state_shape = (bsz,) + state_in.shape
    return pl.pallas_call(
        body,
        grid=(bsz, seq // tm),
        in_specs=[tok_spec, _const_spec(state_in.shape)] + [_const_spec(a.shape) for a in consts],
        out_specs=[tok_spec, pl.BlockSpec((1,) + state_in.shape, lambda b, t: (b, 0, 0, 0))],
        out_shape=[jax.ShapeDtypeStruct(h.shape, h.dtype),
                   jax.ShapeDtypeStruct(state_shape, state_in.dtype)],
        scratch_shapes=scratch,
        compiler_params=pltpu.CompilerParams(
            dimension_semantics=("arbitrary", "arbitrary"), vmem_limit_bytes=VMEM_LIMIT),
        name=name,
    )(h, state_in, *consts)


def _layer_a(h, state_in, g, w_in, cw, cb, w_out, *, tm, name):
    n_cols = cw.shape[0]
    body = functools.partial(_layer_a_kernel, tm=tm, c_inner=n_cols * LANES)
    scratch = [pltpu.VMEM((n_cols, HIST_A + tm, LANES), jnp.float32)]
    return _run_layer(body, h, state_in, (g, w_in, cw, cb, w_out), scratch, tm=tm, name=name)


def _layer_b(h, state_in, g, w_in, cw, cb, ln_g, ln_b, w_out, fg, *, tm, row_block,
             final_norm, name):
    n_cols = cb.shape[0]
    body = functools.partial(_layer_b_kernel, tm=tm, c_inner=n_cols * LANES,
                             row_block=row_block, final_norm=final_norm)
    scratch = [pltpu.VMEM((n_cols // 2, HIST_B + tm, LANES), jnp.uint32),
               pltpu.VMEM((n_cols, tm, LANES), jnp.float32),
               pltpu.VMEM((2, tm, LANES), jnp.float32),
               pltpu.VMEM((tm, n_cols * LANES), jnp.float32),
               pltpu.VMEM((tm, n_cols * LANES), jnp.bfloat16)]
    return _run_layer(body, h, state_in, (g, w_in, cw, cb, ln_g, ln_b, w_out, fg), scratch,
                      tm=tm, name=name)


def _by_lane_col(v):
    lead = v.shape[:-1]
    v = v.reshape(lead + (v.shape[-1] // LANES, LANES))
    return jnp.moveaxis(v, -2, 0)


def _pack_conv_taps(cw):
    bits = lax.bitcast_convert_type(_by_lane_col(cw), jnp.uint32)
    return _pack_bits(bits[0::2], bits[1::2])


def kernel(x, meta, norm_g, a_w_in, a_conv_w, a_conv_b, a_w_out, b_w_in, b_conv_w, b_conv_b,
           b_ln_g, b_ln_b, b_w_out, final_g):
    n_meta = meta.shape[0]
    c_inner = a_conv_w.shape[-1]
    n_cols = c_inner // LANES
    bf16 = jnp.bfloat16
    row = lambda v: v.reshape(1, -1)
    col_row = lambda v: _by_lane_col(row(v))

    a_consts = (row(norm_g[0]), a_w_in[0].astype(bf16), _by_lane_col(a_conv_w[0]),
                col_row(a_conv_b[0]), a_w_out[0].astype(bf16))
    b_consts = (row(norm_g[1]), b_w_in[0].astype(bf16), _pack_conv_taps(b_conv_w[0]),
                col_row(b_conv_b[0]), col_row(b_ln_g[0]), col_row(b_ln_b[0]),
                b_w_out[0].astype(bf16), row(final_g))

    zeros_a = jnp.zeros((n_cols, HIST_A, LANES), jnp.float32)
    zeros_b = jnp.zeros((n_cols // 2, HIST_B, LANES), jnp.uint32)
    meta_h = meta.astype(x.dtype)[None]
    meta_h1, meta_state_a = _layer_a(meta_h, zeros_a, *a_consts, tm=n_meta, name="meta_layer_a")
    _, meta_state_b = _layer_b(meta_h1, zeros_b, *b_consts, tm=n_meta, row_block=n_meta,
                               final_norm=False, name="meta_layer_b")

    h1, _ = _layer_a(x, meta_state_a[0], *a_consts, tm=TILE_A, name="layer_a")
    out, _ = _layer_b(h1, meta_state_b[0], *b_consts, tm=TILE_B, row_block=32,
                      final_norm=True, name="layer_b")
    return out
```

```python
import functools

import jax
import jax.numpy as jnp
from jax import lax
from jax.experimental import pallas as pl
from jax.experimental.pallas import tpu as pltpu

RMS_EPS = 1e-6
LN_EPS = 1e-5

SUBLANES = 8
LANES = 128
HIST_A = 8
HIST_B = 32
CHUNK = 512
CHUNK_COLS = CHUNK // LANES
CHUNK_B = 512
TILE_A = 512
TILE_B = 512
VMEM_LIMIT = 56 * 1024 * 1024


def _rmsnorm(x, g):
    return x * lax.rsqrt(jnp.mean(x * x, axis=-1, keepdims=True) + RMS_EPS) * g


def _silu(x):
    return x * jax.nn.sigmoid(x)


def _lane_col(v, l):
    return v[:, l * LANES:(l + 1) * LANES]


def _init_history(t, ubuf, state_in_ref, hist):
    @pl.when(t == 0)
    def _():
        ubuf[:, 0:hist, :] = state_in_ref[...]


def _roll_history(ubuf, state_out_ref, tm, hist):
    last = ubuf[:, tm:tm + hist, :]
    ubuf[:, 0:hist, :] = last
    state_out_ref[0] = last


def _layer_a_kernel(h_ref, state_in_ref, g_ref, w_in_ref, cw_ref, cb_ref, w_out_ref,
                    out_ref, state_out_ref, ubuf, *, tm, c_inner):
    t = pl.program_id(1)
    _init_history(t, ubuf, state_in_ref, HIST_A)
    k_taps = cw_ref.shape[1]
    h = h_ref[0]
    hn = _rmsnorm(h, g_ref[...]).astype(jnp.bfloat16)
    y = jnp.zeros_like(h)
    for j in range(c_inner // CHUNK):

        def proj(p):
            w = w_in_ref[:, p * c_inner + j * CHUNK:p * c_inner + (j + 1) * CHUNK]
            return jnp.dot(hn, w, preferred_element_type=jnp.float32)

        bg, cg, xv, z = proj(0), proj(1), proj(2), proj(3)
        cx = cg * xv
        gated = []
        for l in range(CHUNK_COLS):
            col = j * CHUNK_COLS + l
            ubuf[col, HIST_A:HIST_A + tm, :] = _lane_col(cx, l)
            conv = jnp.broadcast_to(cb_ref[col], (tm, LANES))
            for k in range(k_taps):
                off = HIST_A - (k_taps - 1) + k
                conv = conv + ubuf[col, off:off + tm, :] * cw_ref[col, k:k + 1, :]
            gated.append(_silu(_lane_col(z, l)) * (_lane_col(bg, l) * conv))
        gated = jnp.concatenate(gated, axis=-1).astype(jnp.bfloat16)
        y = y + jnp.dot(gated, w_out_ref[j * CHUNK:(j + 1) * CHUNK, :],
                        preferred_element_type=jnp.float32)
    out_ref[0] = h + y
    _roll_history(ubuf, state_out_ref, tm, HIST_A)


def _bf16_round_bits(bits):
    return bits + jnp.uint32(0x7FFF) + ((bits >> 16) & jnp.uint32(1))


def _pack_bits(lo_bits, hi_bits):
    return (_bf16_round_bits(lo_bits) >> 16) | (_bf16_round_bits(hi_bits) & jnp.uint32(0xFFFF0000))


def _pack_pair(lo, hi):
    as_bits = lambda v: pltpu.bitcast(v.astype(jnp.bfloat16).astype(jnp.float32), jnp.uint32)
    return (as_bits(lo) >> 16) | as_bits(hi)


def _unpack_pair(word):
    lo = pltpu.bitcast(word << 16, jnp.float32)
    hi = pltpu.bitcast(word & jnp.uint32(0xFFFF0000), jnp.float32)
    return lo, hi


def _conv_pair(ubuf, cw_ref, cb_ref, cbuf, stat, pair, *, tm, row_block):
    k_taps = cw_ref.shape[1]
    n_sub = row_block // SUBLANES
    for rb in range(tm // row_block):
        acc = [None] * n_sub
        for k in range(k_taps):
            w = pltpu.bitcast(jnp.broadcast_to(cw_ref[pair, k:k + 1, :], (SUBLANES, LANES)),
                              jnp.bfloat16)
            for i in range(n_sub):
                off = rb * row_block + i * SUBLANES + HIST_B - (k_taps - 1) + k
                prod = pltpu.bitcast(ubuf[pair, off:off + SUBLANES, :], jnp.bfloat16) * w
                acc[i] = prod if acc[i] is None else acc[i] + prod
        for i in range(n_sub):
            rows = slice(rb * row_block + i * SUBLANES, rb * row_block + (i + 1) * SUBLANES)
            for half, c in enumerate(_unpack_pair(pltpu.bitcast(acc[i], jnp.uint32))):
                col = 2 * pair + half
                c = c + cb_ref[col]
                cbuf[col, rows, :] = c
                if col == 0:
                    stat[0, rows, :] = c
                    stat[1, rows, :] = c * c
                else:
                    stat[0, rows, :] += c
                    stat[1, rows, :] += c * c


def _layer_b_kernel(h_ref, state_in_ref, g_ref, w_in_ref, cw_ref, cb_ref, ln_g_ref, ln_b_ref,
                    w_out_ref, fg_ref, out_ref, state_out_ref, ubuf, cbuf, stat, zbuf, gbuf,
                    *, tm, c_inner, row_block, final_norm):
    t = pl.program_id(1)
    _init_history(t, ubuf, state_in_ref, HIST_B)
    h = h_ref[0]
    hn = _rmsnorm(h, g_ref[...]).astype(jnp.bfloat16)
    n_chunks = c_inner // CHUNK_B
    chunk_pairs = CHUNK_B // (2 * LANES)

    def proj(p, j):
        w = w_in_ref[:, p * c_inner + j * CHUNK_B:p * c_inner + (j + 1) * CHUNK_B]
        return jnp.dot(hn, w, preferred_element_type=jnp.float32)

    for j in range(n_chunks):
        u = proj(0, j) * jax.nn.sigmoid(proj(1, j))
        for p in range(chunk_pairs):
            ubuf[j * chunk_pairs + p, HIST_B:HIST_B + tm, :] = _pack_pair(
                _lane_col(u, 2 * p), _lane_col(u, 2 * p + 1))
        zbuf[:, j * CHUNK_B:(j + 1) * CHUNK_B] = _silu(proj(2, j))
        for p in range(chunk_pairs):
            _conv_pair(ubuf, cw_ref, cb_ref, cbuf, stat, j * chunk_pairs + p,
                       tm=tm, row_block=row_block)

    mu = jnp.sum(stat[0], axis=-1, keepdims=True) / c_inner
    var = jnp.maximum(jnp.sum(stat[1], axis=-1, keepdims=True) / c_inner - mu * mu, 0.0)
    rstd = lax.rsqrt(var + LN_EPS)

    for col in range(c_inner // LANES):
        lanes = slice(col * LANES, (col + 1) * LANES)
        normed = (cbuf[col] - mu) * rstd * ln_g_ref[col] + ln_b_ref[col]
        gbuf[:, lanes] = (zbuf[:, lanes] * _silu(normed)).astype(jnp.bfloat16)
    out = h + jnp.dot(gbuf[...], w_out_ref[...], preferred_element_type=jnp.float32)
    if final_norm:
        out = _rmsnorm(out, fg_ref[...])
    out_ref[0] = out
    _roll_history(ubuf, state_out_ref, tm, HIST_B)


def _const_spec(shape):
    return pl.BlockSpec(shape, lambda b, t: (0,) * len(shape), pipeline_mode=pl.Buffered(1))


def _run_layer(body, h, state_in, consts, scratch, *, tm, name):
    bsz, seq, d = h.shape
    tok_spec = pl.BlockSpec((1, tm, d), lambda b, t: (b, t, 0))
    state_shape = (bsz,) + state_in.shape
    return pl.pallas_call(
        body,
        grid=(bsz, seq // tm),
        in_specs=[tok_spec, _const_spec(state_in.shape)] + [_const_spec(a.shape) for a in consts],
        out_specs=[tok_spec, pl.BlockSpec((1,) + state_in.shape, lambda b, t: (b, 0, 0, 0))],
        out_shape=[jax.ShapeDtypeStruct(h.shape, h.dtype),
                   jax.ShapeDtypeStruct(state_shape, state_in.dtype)],
        scratch_shapes=scratch,
        compiler_params=pltpu.CompilerParams(
            dimension_semantics=("arbitrary", "arbitrary"), vmem_limit_bytes=VMEM_LIMIT),
        name=name,
    )(h, state_in, *consts)


def _layer_a(h, state_in, g, w_in, cw, cb, w_out, *, tm, name):
    n_cols = cw.shape[0]
    body = functools.partial(_layer_a_kernel, tm=tm, c_inner=n_cols * LANES)
    scratch = [pltpu.VMEM((n_cols, HIST_A + tm, LANES), jnp.float32)]
    return _run_layer(body, h, state_in, (g, w_in, cw, cb, w_out), scratch, tm=tm, name=name)


def _layer_b(h, state_in, g, w_in, cw, cb, ln_g, ln_b, w_out, fg, *, tm, row_block,
             final_norm, name):
    n_cols = cb.shape[0]
    body = functools.partial(_layer_b_kernel, tm=tm, c_inner=n_cols * LANES,
                             row_block=row_block, final_norm=final_norm)
    scratch = [pltpu.VMEM((n_cols // 2, HIST_B + tm, LANES), jnp.uint32),
               pltpu.VMEM((n_cols, tm, LANES), jnp.float32),
               pltpu.VMEM((2, tm, LANES), jnp.float32),
               pltpu.VMEM((tm, n_cols * LANES), jnp.float32),
               pltpu.VMEM((tm, n_cols * LANES), jnp.bfloat16)]
    return _run_layer(body, h, state_in, (g, w_in, cw, cb, ln_g, ln_b, w_out, fg), scratch,
                      tm=tm, name=name)


def _by_lane_col(v):
    lead = v.shape[:-1]
    v = v.reshape(lead + (v.shape[-1] // LANES, LANES))
    return jnp.moveaxis(v, -2, 0)


def _pack_conv_taps(cw):
    bits = lax.bitcast_convert_type(_by_lane_col(cw), jnp.uint32)
    return _pack_bits(bits[0::2], bits[1::2])


def kernel(x, meta, norm_g, a_w_in, a_conv_w, a_conv_b, a_w_out, b_w_in, b_conv_w, b_conv_b,
           b_ln_g, b_ln_b, b_w_out, final_g):
    n_meta = meta.shape[0]
    c_inner = a_conv_w.shape[-1]
    n_cols = c_inner // LANES
    bf16 = jnp.bfloat16
    row = lambda v: v.reshape(1, -1)
    col_row = lambda v: _by_lane_col(row(v))

    a_consts = (row(norm_g[0]), a_w_in[0].astype(bf16), _by_lane_col(a_conv_w[0]),
                col_row(a_conv_b[0]), a_w_out[0].astype(bf16))
    b_consts = (row(norm_g[1]), b_w_in[0].astype(bf16), _pack_conv_taps(b_conv_w[0]),
                col_row(b_conv_b[0]), col_row(b_ln_g[0]), col_row(b_ln_b[0]),
                b_w_out[0].astype(bf16), row(final_g))

    zeros_a = jnp.zeros((n_cols, HIST_A, LANES), jnp.float32)
    zeros_b = jnp.zeros((n_cols // 2, HIST_B, LANES), jnp.uint32)
    meta_h = meta.astype(x.dtype)[None]
    meta_h1, meta_state_a = _layer_a(meta_h, zeros_a, *a_consts, tm=n_meta, name="meta_layer_a")
    _, meta_state_b = _layer_b(meta_h1, zeros_b, *b_consts, tm=n_meta, row_block=n_meta,
                               final_norm=False, name="meta_layer_b")

    h1, _ = _layer_a(x, meta_state_a[0], *a_consts, tm=TILE_A, name="layer_a")
    out, _ = _layer_b(h1, meta_state_b[0], *b_consts, tm=TILE_B, row_block=32,
                      final_norm=True, name="layer_b")
    return out
```

```python
import functools

import jax
import jax.numpy as jnp
from jax import lax
from jax.experimental import pallas as pl
from jax.experimental.pallas import tpu as pltpu

RMS_EPS = 1e-6
LN_EPS = 1e-5

SUBLANES = 8
LANES = 128
HIST_A = 8
HIST_B = 32
CHUNK = 512
CHUNK_COLS = CHUNK // LANES
CHUNK_B = 512
TILE_A = 512
TILE_B = 512
VMEM_LIMIT = 56 * 1024 * 1024


def _rmsnorm(x, g):
    return x * lax.rsqrt(jnp.mean(x * x, axis=-1, keepdims=True) + RMS_EPS) * g


def _silu(x):
    return x * jax.nn.sigmoid(x)


def _lane_col(v, l):
    return v[:, l * LANES:(l + 1) * LANES]


def _init_history(t, ubuf, state_in_ref, hist):
    @pl.when(t == 0)
    def _():
        ubuf[:, 0:hist, :] = state_in_ref[...]


def _roll_history(ubuf, state_out_ref, tm, hist):
    last = ubuf[:, tm:tm + hist, :]
    ubuf[:, 0:hist, :] = last
    state_out_ref[0] = last


def _layer_a_kernel(h_ref, state_in_ref, g_ref, w_in_ref, cw_ref, cb_ref, w_out_ref,
                    out_ref, state_out_ref, ubuf, gbuf, *, tm, c_inner):
    t = pl.program_id(1)
    _init_history(t, ubuf, state_in_ref, HIST_A)
    k_taps = cw_ref.shape[1]
    h = h_ref[0]
    hn = _rmsnorm(h, g_ref[...]).astype(jnp.bfloat16)
    for j in range(c_inner // CHUNK):

        def proj(p):
            w = w_in_ref[:, p * c_inner + j * CHUNK:p * c_inner + (j + 1) * CHUNK]
            return jnp.dot(hn, w, preferred_element_type=jnp.float32)

        bg, cg, xv, z = proj(0), proj(1), proj(2), proj(3)
        cx = cg * xv
        gated = []
        for l in range(CHUNK_COLS):
            col = j * CHUNK_COLS + l
            ubuf[col, HIST_A:HIST_A + tm, :] = _lane_col(cx, l)
            conv = jnp.broadcast_to(cb_ref[col], (tm, LANES))
            for k in range(k_taps):
                off = HIST_A - (k_taps - 1) + k
                conv = conv + ubuf[col, off:off + tm, :] * cw_ref[col, k:k + 1, :]
            gated.append(_silu(_lane_col(z, l)) * (_lane_col(bg, l) * conv))
        gbuf[:, j * CHUNK:(j + 1) * CHUNK] = jnp.concatenate(gated, axis=-1).astype(jnp.bfloat16)
    out_ref[0] = h + jnp.dot(gbuf[...], w_out_ref[...], preferred_element_type=jnp.float32)
    _roll_history(ubuf, state_out_ref, tm, HIST_A)


def _bf16_round_bits(bits):
    return bits + jnp.uint32(0x7FFF) + ((bits >> 16) & jnp.uint32(1))


def _pack_bits(lo_bits, hi_bits):
    return (_bf16_round_bits(lo_bits) >> 16) | (_bf16_round_bits(hi_bits) & jnp.uint32(0xFFFF0000))


def _pack_pair(lo, hi):
    as_bits = lambda v: pltpu.bitcast(v.astype(jnp.bfloat16).astype(jnp.float32), jnp.uint32)
    return (as_bits(lo) >> 16) | as_bits(hi)


def _unpack_pair(word):
    lo = pltpu.bitcast(word << 16, jnp.float32)
    hi = pltpu.bitcast(word & jnp.uint32(0xFFFF0000), jnp.float32)
    return lo, hi


def _conv_pair(ubuf, cw_ref, cb_ref, cbuf, stat, pair, *, tm, row_block):
    k_taps = cw_ref.shape[1]
    n_sub = row_block // SUBLANES
    for rb in range(tm // row_block):
        acc = [None] * n_sub
        for k in range(k_taps):
            w = pltpu.bitcast(jnp.broadcast_to(cw_ref[pair, k:k + 1, :], (SUBLANES, LANES)),
                              jnp.bfloat16)
            for i in range(n_sub):
                off = rb * row_block + i * SUBLANES + HIST_B - (k_taps - 1) + k
                prod = pltpu.bitcast(ubuf[pair, off:off + SUBLANES, :], jnp.bfloat16) * w
                acc[i] = prod if acc[i] is None else acc[i] + prod
        for i in range(n_sub):
            rows = slice(rb * row_block + i * SUBLANES, rb * row_block + (i + 1) * SUBLANES)
            lo, hi = _unpack_pair(pltpu.bitcast(acc[i], jnp.uint32))
            lo = lo + cb_ref[2 * pair]
            hi = hi + cb_ref[2 * pair + 1]
            cbuf[2 * pair, rows, :] = lo
            cbuf[2 * pair + 1, rows, :] = hi
            if pair == 0:
                stat[0, rows, :] = lo + hi
                stat[1, rows, :] = lo * lo + hi * hi
            else:
                stat[0, rows, :] += lo + hi
                stat[1, rows, :] += lo * lo + hi * hi


def _layer_b_kernel(h_ref, state_in_ref, g_ref, w_in_ref, cw_ref, cb_ref, ln_g_ref, ln_b_ref,
                    w_out_ref, fg_ref, out_ref, state_out_ref, ubuf, cbuf, stat, zbuf, gbuf,
                    *, tm, c_inner, row_block, final_norm):
    t = pl.program_id(1)
    _init_history(t, ubuf, state_in_ref, HIST_B)
    h = h_ref[0]
    hn = _rmsnorm(h, g_ref[...]).astype(jnp.bfloat16)
    n_chunks = c_inner // CHUNK_B
    chunk_pairs = CHUNK_B // (2 * LANES)

    def proj(p, j):
        w = w_in_ref[:, p * c_inner + j * CHUNK_B:p * c_inner + (j + 1) * CHUNK_B]
        return jnp.dot(hn, w, preferred_element_type=jnp.float32)

    for j in range(n_chunks):
        u = proj(0, j) * jax.nn.sigmoid(proj(1, j))
        for p in range(chunk_pairs):
            ubuf[j * chunk_pairs + p, HIST_B:HIST_B + tm, :] = _pack_pair(
                _lane_col(u, 2 * p), _lane_col(u, 2 * p + 1))
        zbuf[:, j * CHUNK_B:(j + 1) * CHUNK_B] = _silu(proj(2, j))
        for p in range(chunk_pairs):
            _conv_pair(ubuf, cw_ref, cb_ref, cbuf, stat, j * chunk_pairs + p,
                       tm=tm, row_block=row_block)

    mu = jnp.sum(stat[0], axis=-1, keepdims=True) / c_inner
    var = jnp.maximum(jnp.sum(stat[1], axis=-1, keepdims=True) / c_inner - mu * mu, 0.0)
    rstd = lax.rsqrt(var + LN_EPS)

    for col in range(c_inner // LANES):
        lanes = slice(col * LANES, (col + 1) * LANES)
        normed = (cbuf[col] - mu) * rstd * ln_g_ref[col] + ln_b_ref[col]
        gbuf[:, lanes] = (zbuf[:, lanes] * _silu(normed)).astype(jnp.bfloat16)
    out = h + jnp.dot(gbuf[...], w_out_ref[...], preferred_element_type=jnp.float32)
    if final_norm:
        out = _rmsnorm(out, fg_ref[...])
    out_ref[0] = out
    _roll_history(ubuf, state_out_ref, tm, HIST_B)


def _const_spec(shape):
    return pl.BlockSpec(shape, lambda b, t: (0,) * len(shape), pipeline_mode=pl.Buffered(1))


def _run_layer(body, h, state_in, consts, scratch, *, tm, name):
    bsz, seq, d = h.shape
    tok_spec = pl.BlockSpec((1, tm, d), lambda b, t: (b, t, 0))
    state_shape = (bsz,) + state_in.shape
    return pl.pallas_call(
        body,
        grid=(bsz, seq // tm),
        in_specs=[tok_spec, _const_spec(state_in.shape)] + [_const_spec(a.shape) for a in consts],
        out_specs=[tok_spec, pl.BlockSpec((1,) + state_in.shape, lambda b, t: (b, 0, 0, 0))],
        out_shape=[jax.ShapeDtypeStruct(h.shape, h.dtype),
                   jax.ShapeDtypeStruct(state_shape, state_in.dtype)],
        scratch_shapes=scratch,
        compiler_params=pltpu.CompilerParams(
            dimension_semantics=("arbitrary", "arbitrary"), vmem_limit_bytes=VMEM_LIMIT),
        name=name,
    )(h, state_in, *consts)


def _layer_a(h, state_in, g, w_in, cw, cb, w_out, *, tm, name):
    n_cols = cw.shape[0]
    body = functools.partial(_layer_a_kernel, tm=tm, c_inner=n_cols * LANES)
    scratch = [pltpu.VMEM((n_cols, HIST_A + tm, LANES), jnp.float32),
               pltpu.VMEM((tm, n_cols * LANES), jnp.bfloat16)]
    return _run_layer(body, h, state_in, (g, w_in, cw, cb, w_out), scratch, tm=tm, name=name)


def _layer_b(h, state_in, g, w_in, cw, cb, ln_g, ln_b, w_out, fg, *, tm, row_block,
             final_norm, name):
    n_cols = cb.shape[0]
    body = functools.partial(_layer_b_kernel, tm=tm, c_inner=n_cols * LANES,
                             row_block=row_block, final_norm=final_norm)
    scratch = [pltpu.VMEM((n_cols // 2, HIST_B + tm, LANES), jnp.uint32),
               pltpu.VMEM((n_cols, tm, LANES), jnp.float32),
               pltpu.VMEM((2, tm, LANES), jnp.float32),
               pltpu.VMEM((tm, n_cols * LANES), jnp.float32),
               pltpu.VMEM((tm, n_cols * LANES), jnp.bfloat16)]
    return _run_layer(body, h, state_in, (g, w_in, cw, cb, ln_g, ln_b, w_out, fg), scratch,
                      tm=tm, name=name)


def _by_lane_col(v):
    lead = v.shape[:-1]
    v = v.reshape(lead + (v.shape[-1] // LANES, LANES))
    return jnp.moveaxis(v, -2, 0)


def _pack_conv_taps(cw):
    bits = lax.bitcast_convert_type(_by_lane_col(cw), jnp.uint32)
    return _pack_bits(bits[0::2], bits[1::2])


def kernel(x, meta, norm_g, a_w_in, a_conv_w, a_conv_b, a_w_out, b_w_in, b_conv_w, b_conv_b,
           b_ln_g, b_ln_b, b_w_out, final_g):
    n_meta = meta.shape[0]
    c_inner = a_conv_w.shape[-1]
    n_cols = c_inner // LANES
    bf16 = jnp.bfloat16
    row = lambda v: v.reshape(1, -1)
    col_row = lambda v: _by_lane_col(row(v))

    a_consts = (row(norm_g[0]), a_w_in[0].astype(bf16), _by_lane_col(a_conv_w[0]),
                col_row(a_conv_b[0]), a_w_out[0].astype(bf16))
    b_consts = (row(norm_g[1]), b_w_in[0].astype(bf16), _pack_conv_taps(b_conv_w[0]),
                col_row(b_conv_b[0]), col_row(b_ln_g[0]), col_row(b_ln_b[0]),
                b_w_out[0].astype(bf16), row(final_g))

    zeros_a = jnp.zeros((n_cols, HIST_A, LANES), jnp.float32)
    zeros_b = jnp.zeros((n_cols // 2, HIST_B, LANES), jnp.uint32)
    meta_h = meta.astype(x.dtype)[None]
    meta_h1, meta_state_a = _layer_a(meta_h, zeros_a, *a_consts, tm=n_meta, name="meta_layer_a")
    _, meta_state_b = _layer_b(meta_h1, zeros_b, *b_consts, tm=n_meta, row_block=n_meta,
                               final_norm=False, name="meta_layer_b")

    h1, _ = _layer_a(x, meta_state_a[0], *a_consts, tm=TILE_A, name="layer_a")
    out, _ = _layer_b(h1, meta_state_b[0], *b_consts, tm=TILE_B, row_block=32,
                      final_norm=True, name="layer_b")
    return out
```

```python
import functools

import jax
import jax.numpy as jnp
from jax import lax
from jax.experimental import pallas as pl
from jax.experimental.pallas import tpu as pltpu

RMS_EPS = 1e-6
LN_EPS = 1e-5

SUBLANES = 8
LANES = 128
HIST_A = 32
HIST_B = 32
CHUNK = 512
CHUNK_COLS = CHUNK // LANES
CHUNK_B = 512
TILE_A = 512
TILE_B = 512
VMEM_LIMIT = 56 * 1024 * 1024


def _rmsnorm(x, g):
    return x * lax.rsqrt(jnp.mean(x * x, axis=-1, keepdims=True) + RMS_EPS) * g


def _silu(x):
    return x * jax.nn.sigmoid(x)


def _lane_col(v, l):
    return v[:, l * LANES:(l + 1) * LANES]


def _init_history(t, ubuf, state_in_ref, hist):
    @pl.when(t == 0)
    def _():
        ubuf[:, 0:hist, :] = state_in_ref[...]


def _roll_history(ubuf, state_out_ref, tm, hist):
    last = ubuf[:, tm:tm + hist, :]
    ubuf[:, 0:hist, :] = last
    state_out_ref[0] = last


def _layer_a_kernel(h_ref, state_in_ref, g_ref, w_in_ref, cw_ref, cb_ref, w_out_ref,
                    out_ref, state_out_ref, ubuf, gbuf, *, tm, c_inner):
    t = pl.program_id(1)
    _init_history(t, ubuf, state_in_ref, HIST_A)
    k_taps = cw_ref.shape[1]
    h = h_ref[0]
    hn = _rmsnorm(h, g_ref[...]).astype(jnp.bfloat16)
    for j in range(c_inner // CHUNK):

        def proj(p):
            w = w_in_ref[:, p * c_inner + j * CHUNK:p * c_inner + (j + 1) * CHUNK]
            return jnp.dot(hn, w, preferred_element_type=jnp.float32)

        bg, cg, xv, z = proj(0), proj(1), proj(2), proj(3)
        cx = cg * xv
        gated = []
        for l in range(CHUNK_COLS):
            col = j * CHUNK_COLS + l
            ubuf[col, HIST_A:HIST_A + tm, :] = _lane_col(cx, l)
            conv = jnp.broadcast_to(cb_ref[col], (tm, LANES))
            for k in range(k_taps):
                off = HIST_A - (k_taps - 1) + k
                conv = conv + ubuf[col, off:off + tm, :] * cw_ref[col, k:k + 1, :]
            gated.append(_silu(_lane_col(z, l)) * (_lane_col(bg, l) * conv))
        gbuf[:, j * CHUNK:(j + 1) * CHUNK] = jnp.concatenate(gated, axis=-1).astype(jnp.bfloat16)
    out_ref[0] = h + jnp.dot(gbuf[...], w_out_ref[...], preferred_element_type=jnp.float32)
    _roll_history(ubuf, state_out_ref, tm, HIST_A)


def _bf16_round_bits(bits):
    return bits + jnp.uint32(0x7FFF) + ((bits >> 16) & jnp.uint32(1))


def _pack_bits(lo_bits, hi_bits):
    return (_bf16_round_bits(lo_bits) >> 16) | (_bf16_round_bits(hi_bits) & jnp.uint32(0xFFFF0000))


def _pack_pair(lo, hi):
    as_bits = lambda v: pltpu.bitcast(v.astype(jnp.bfloat16).astype(jnp.float32), jnp.uint32)
    return (as_bits(lo) >> 16) | as_bits(hi)


def _unpack_pair(word):
    lo = pltpu.bitcast(word << 16, jnp.float32)
    hi = pltpu.bitcast(word & jnp.uint32(0xFFFF0000), jnp.float32)
    return lo, hi


def _conv_pair(ubuf, cw_ref, cb_ref, cbuf, stat, pair, *, tm, row_block):
    k_taps = cw_ref.shape[1]
    n_sub = row_block // SUBLANES
    for rb in range(tm // row_block):
        acc = [None] * n_sub
        for k in range(k_taps):
            w = pltpu.bitcast(jnp.broadcast_to(cw_ref[pair, k:k + 1, :], (SUBLANES, LANES)),
                              jnp.bfloat16)
            for i in range(n_sub):
                off = rb * row_block + i * SUBLANES + HIST_B - (k_taps - 1) + k
                prod = pltpu.bitcast(ubuf[pair, off:off + SUBLANES, :], jnp.bfloat16) * w
                acc[i] = prod if acc[i] is None else acc[i] + prod
        for i in range(n_sub):
            rows = slice(rb * row_block + i * SUBLANES, rb * row_block + (i + 1) * SUBLANES)
            lo, hi = _unpack_pair(pltpu.bitcast(acc[i], jnp.uint32))
            lo = lo + cb_ref[2 * pair]
            hi = hi + cb_ref[2 * pair + 1]
            cbuf[2 * pair, rows, :] = lo
            cbuf[2 * pair + 1, rows, :] = hi
            if pair == 0:
                stat[0, rows, :] = lo + hi
                stat[1, rows, :] = lo * lo + hi * hi
            else:
                stat[0, rows, :] += lo + hi
                stat[1, rows, :] += lo * lo + hi * hi


def _layer_b_kernel(h_ref, state_in_ref, g_ref, w_in_ref, cw_ref, cb_ref, ln_g_ref, ln_b_ref,
                    w_out_ref, fg_ref, out_ref, state_out_ref, ubuf, cbuf, stat, zbuf, gbuf,
                    *, tm, c_inner, row_block, final_norm):
    t = pl.program_id(1)
    _init_history(t, ubuf, state_in_ref, HIST_B)
    h = h_ref[0]
    hn = _rmsnorm(h, g_ref[...]).astype(jnp.bfloat16)
    n_chunks = c_inner // CHUNK_B
    chunk_pairs = CHUNK_B // (2 * LANES)

    def proj(p, j):
        w = w_in_ref[:, p * c_inner + j * CHUNK_B:p * c_inner + (j + 1) * CHUNK_B]
        return jnp.dot(hn, w, preferred_element_type=jnp.float32)

    for j in range(n_chunks):
        u = proj(0, j) * jax.nn.sigmoid(proj(1, j))
        for p in range(chunk_pairs):
            ubuf[j * chunk_pairs + p, HIST_B:HIST_B + tm, :] = _pack_pair(
                _lane_col(u, 2 * p), _lane_col(u, 2 * p + 1))
        zbuf[:, j * CHUNK_B:(j + 1) * CHUNK_B] = _silu(proj(2, j))
        for p in range(chunk_pairs):
            _conv_pair(ubuf, cw_ref, cb_ref, cbuf, stat, j * chunk_pairs + p,
                       tm=tm, row_block=row_block)

    mu = jnp.sum(stat[0], axis=-1, keepdims=True) / c_inner
    var = jnp.maximum(jnp.sum(stat[1], axis=-1, keepdims=True) / c_inner - mu * mu, 0.0)
    rstd = lax.rsqrt(var + LN_EPS)

    for col in range(c_inner // LANES):
        lanes = slice(col * LANES, (col + 1) * LANES)
        normed = (cbuf[col] - mu) * rstd * ln_g_ref[col] + ln_b_ref[col]
        gbuf[:, lanes] = (zbuf[:, lanes] * _silu(normed)).astype(jnp.bfloat16)
    out = h + jnp.dot(gbuf[...], w_out_ref[...], preferred_element_type=jnp.float32)
    if final_norm:
        out = _rmsnorm(out, fg_ref[...])
    out_ref[0] = out
    _roll_history(ubuf, state_out_ref, tm, HIST_B)


def _const_spec(shape):
    return pl.BlockSpec(shape, lambda b, t: (0,) * len(shape), pipeline_mode=pl.Buffered(1))


def _run_layer(body, h, state_in, consts, scratch, *, tm, name):
    bsz, seq, d = h.shape
    tok_spec = pl.BlockSpec((1, tm, d), lambda b, t: (b, t, 0))
    state_shape = (bsz,) + state_in.shape
    return pl.pallas_call(
        body,
        grid=(bsz, seq // tm),
        in_specs=[tok_spec, _const_spec(state_in.shape)] + [_const_spec(a.shape) for a in consts],
        out_specs=[tok_spec, pl.BlockSpec((1,) + state_in.shape, lambda b, t: (b, 0, 0, 0))],
        out_shape=[jax.ShapeDtypeStruct(h.shape, h.dtype),
                   jax.ShapeDtypeStruct(state_shape, state_in.dtype)],
        scratch_shapes=scratch,
        compiler_params=pltpu.CompilerParams(
            dimension_semantics=("arbitrary", "arbitrary"), vmem_limit_bytes=VMEM_LIMIT),
        name=name,
    )(h, state_in, *consts)


def _layer_a(h, state_in, g, w_in, cw, cb, w_out, *, tm, name):
    n_cols = cw.shape[0]
    body = functools.partial(_layer_a_kernel, tm=tm, c_inner=n_cols * LANES)
    scratch = [pltpu.VMEM((n_cols, HIST_A + tm, LANES), jnp.float32),
               pltpu.VMEM((tm, n_cols * LANES), jnp.bfloat16)]
    return _run_layer(body, h, state_in, (g, w_in, cw, cb, w_out), scratch, tm=tm, name=name)


def _layer_b(h, state_in, g, w_in, cw, cb, ln_g, ln_b, w_out, fg, *, tm, row_block,
             final_norm, name):
    n_cols = cb.shape[0]
    body = functools.partial(_layer_b_kernel, tm=tm, c_inner=n_cols * LANES,
                             row_block=row_block, final_norm=final_norm)
    scratch = [pltpu.VMEM((n_cols // 2, HIST_B + tm, LANES), jnp.uint32),
               pltpu.VMEM((n_cols, tm, LANES), jnp.float32),
               pltpu.VMEM((2, tm, LANES), jnp.float32),
               pltpu.VMEM((tm, n_cols * LANES), jnp.float32),
               pltpu.VMEM((tm, n_cols * LANES), jnp.bfloat16)]
    return _run_layer(body, h, state_in, (g, w_in, cw, cb, ln_g, ln_b, w_out, fg), scratch,
                      tm=tm, name=name)


def _by_lane_col(v):
    lead = v.shape[:-1]
    v = v.reshape(lead + (v.shape[-1] // LANES, LANES))
    return jnp.moveaxis(v, -2, 0)


def _pack_conv_taps(cw):
    bits = lax.bitcast_convert_type(_by_lane_col(cw), jnp.uint32)
    return _pack_bits(bits[0::2], bits[1::2])


def kernel(x, meta, norm_g, a_w_in, a_conv_w, a_conv_b, a_w_out, b_w_in, b_conv_w, b_conv_b,
           b_ln_g, b_ln_b, b_w_out, final_g):
    n_meta = meta.shape[0]
    c_inner = a_conv_w.shape[-1]
    n_cols = c_inner // LANES
    bf16 = jnp.bfloat16
    row = lambda v: v.reshape(1, -1)
    col_row = lambda v: _by_lane_col(row(v))

    a_consts = (row(norm_g[0]), a_w_in[0].astype(bf16), _by_lane_col(a_conv_w[0]),
                col_row(a_conv_b[0]), a_w_out[0].astype(bf16))
    b_consts = (row(norm_g[1]), b_w_in[0].astype(bf16), _pack_conv_taps(b_conv_w[0]),
                col_row(b_conv_b[0]), col_row(b_ln_g[0]), col_row(b_ln_b[0]),
                b_w_out[0].astype(bf16), row(final_g))

    zeros_a = jnp.zeros((n_cols, HIST_A, LANES), jnp.float32)
    zeros_b = jnp.zeros((n_cols // 2, HIST_B, LANES), jnp.uint32)
    meta_h = meta.astype(x.dtype)[None]
    meta_h1, meta_state_a = _layer_a(meta_h, zeros_a, *a_consts, tm=n_meta, name="meta_layer_a")
    _, meta_state_b = _layer_b(meta_h1, zeros_b, *b_consts, tm=n_meta, row_block=n_meta,
                               final_norm=False, name="meta_layer_b")

    h1, _ = _layer_a(x, meta_state_a[0], *a_consts, tm=TILE_A, name="layer_a")
    out, _ = _layer_b(h1, meta_state_b[0], *b_consts, tm=TILE_B, row_block=32,
                      final_norm=True, name="layer_b")
    return out
```
